```python
import math
import jax, jax.numpy as jnp
from jax import lax
import numpy as np


D_MODEL = 2048
BATCH = 4
SEQ = 4096
DEPTH = 2

PLE_DIM = 256
EPS = 1e-6
SHORT_CONV = 4
ROPE_BASE = 10000.0
AB_CHUNK = 128
RET_HEADS = 4
RET_HEAD_DIM = 256
RET_WIDTH = RET_HEADS * RET_HEAD_DIM
MLSTM_HEADS = 4
MLSTM_HEAD_DIM = 256
MLSTM_WIDTH = MLSTM_HEADS * MLSTM_HEAD_DIM
AB_SIZES = (RET_WIDTH,) * 4 + (MLSTM_WIDTH,) * 4 + (MLSTM_HEADS, MLSTM_HEADS)
AB_IN = sum(AB_SIZES)
SSM_INNER = 2 * D_MODEL
SSM_HEAD_DIM = 64
SSM_HEADS = SSM_INNER // SSM_HEAD_DIM
SSM_GROUPS = 8
SSM_HEADS_PER_GROUP = SSM_HEADS // SSM_GROUPS
SSM_STATE = 128
SSM_CHUNK = 128
SSM_CONV_DIM = SSM_INNER + 2 * SSM_GROUPS * SSM_STATE
SSM_IN = SSM_INNER + SSM_CONV_DIM + SSM_HEADS
PEER_HEADS = 8
PEER_NKEYS = 128
PEER_EXPERTS = PEER_NKEYS * PEER_NKEYS
PEER_DKEY = 256
PEER_TOPK = 16
PEER_TOKEN_BLOCK = 128
N_EVEN = (DEPTH + 1) // 2
N_ODD = DEPTH // 2

kernel_name = 'hybrid_retention_mlstm_ssd_peer'


def rms_norm(x, w):
    xf = x.astype(jnp.float32)
    y = xf * lax.rsqrt(jnp.mean(xf * xf, axis=-1, keepdims=True) + EPS)
    return (y * w.astype(jnp.float32)).astype(x.dtype)


def head_group_norm(y, w):
    B, S = y.shape[:2]
    mu = jnp.mean(y, axis=-1, keepdims=True)
    yc = y - mu
    yn = yc * lax.rsqrt(jnp.mean(yc * yc, axis=-1, keepdims=True) + EPS)
    return yn.reshape(B, S, -1) * w.astype(jnp.float32)


def causal_dwconv(x, w, b):
    K, C = w.shape
    y = lax.conv_general_dilated(x, w[:, None, :].astype(x.dtype), window_strides=(1,),
                                 padding=[(K - 1, 0)], dimension_numbers=('NWC', 'WIO', 'NWC'),
                                 feature_group_count=C)
    return y + b.astype(x.dtype)


def rotary(t, pos):
    half = t.shape[-1] // 2
    inv = ROPE_BASE ** (-jnp.arange(half, dtype=jnp.float32) / half)
    ang = pos.astype(jnp.float32)[:, None] * inv[None, :]
    cos = jnp.cos(ang)[None, :, None, :]
    sin = jnp.sin(ang)[None, :, None, :]
    t1, t2 = t[..., :half], t[..., half:]
    return jnp.concatenate([t1 * cos - t2 * sin, t1 * sin + t2 * cos], axis=-1)


def to_chunks(t, L):
    B, S, H = t.shape[:3]
    t = t.reshape((B, S // L, L, H) + t.shape[3:])
    return jnp.moveaxis(jnp.moveaxis(t, 1, 0), 3, 2)


def from_chunks(t):
    nC, B, H, L, D = t.shape
    return jnp.moveaxis(jnp.moveaxis(t, 2, 3), 0, 1).reshape(B, nC * L, H, D)


def retention(q, k, v):
    B, S, H, Dk = q.shape
    Dv = v.shape[-1]
    L = AB_CHUNK
    log_g = jnp.log1p(-(2.0 ** (-5.0 - jnp.arange(H, dtype=jnp.float32))))
    idx = jnp.arange(L, dtype=jnp.float32)
    rel = idx[:, None] - idx[None, :]
    decay = jnp.where(rel >= 0, jnp.exp(rel[None] * log_g[:, None, None]), 0.0)
    q_decay = jnp.exp((idx + 1.0)[None, :] * log_g[:, None])[None, :, :, None]
    k_decay = jnp.exp((L - 1.0 - idx)[None, :] * log_g[:, None])[None, :, :, None]
    chunk_decay = jnp.exp(L * log_g)[None, :, None, None]
    k = k * (Dk ** -0.5)

    def step(R, inp):
        qi, ki, vi = inp
        inner = jnp.einsum('bhtd,bhsd->bhts', qi, ki) * decay
        o = jnp.einsum('bhts,bhsv->bhtv', inner, vi) + jnp.einsum('bhtd,bhdv->bhtv', qi, R) * q_decay
        R = R * chunk_decay + jnp.einsum('bhsd,bhsv->bhdv', ki * k_decay, vi)
        return R, o

    R0 = jnp.zeros((B, H, Dk, Dv), jnp.float32)
    _, o = lax.scan(step, R0, (to_chunks(q, L), to_chunks(k, L), to_chunks(v, L)))
    return from_chunks(o)


def mlstm(q, k, v, i_pre, f_pre):
    B, S, H, D = q.shape
    L = AB_CHUNK
    k = k * (D ** -0.5)
    log_f = jax.nn.log_sigmoid(f_pre)
    gates = lambda g: jnp.moveaxis(g.reshape(B, S // L, L, H), 1, 0).transpose(0, 1, 3, 2)
    mask = jnp.tril(jnp.ones((L, L), dtype=bool))

    def step(carry, inp):
        C, n, m = carry
        qi, ki, vi, ii, lfi = inp
        b = jnp.cumsum(lfi, axis=-1)
        log_intra = jnp.where(mask, b[..., :, None] - b[..., None, :] + ii[..., None, :], -jnp.inf)
        log_inter = b + m[..., None]
        m_t = jnp.maximum(log_inter, jnp.max(log_intra, axis=-1))
        w_intra = jnp.exp(log_intra - m_t[..., None])
        w_inter = jnp.exp(log_inter - m_t)
        s = jnp.einsum('bhtd,bhsd->bhts', qi, ki) * w_intra
        num = jnp.einsum('bhts,bhsv->bhtv', s, vi) + w_inter[..., None] * jnp.einsum('bhtd,bhdv->bhtv', qi, C)
        den = jnp.sum(s, axis=-1) + w_inter * jnp.einsum('bhtd,bhd->bht', qi, n)
        h = num / jnp.maximum(jnp.abs(den), jnp.exp(-m_t))[..., None]
        m_new = m_t[..., -1]
        w_end = jnp.exp(b[..., -1:] - b + ii - m_new[..., None])
        w_prev = jnp.exp(b[..., -1] + m - m_new)
        kw = ki * w_end[..., None]
        C = w_prev[..., None, None] * C + jnp.einsum('bhsd,bhsv->bhdv', kw, vi)
        n = w_prev[..., None] * n + jnp.sum(kw, axis=2)
        return (C, n, m_new), h

    carry0 = (jnp.zeros((B, H, D, D), jnp.float32), jnp.zeros((B, H, D), jnp.float32),
              jnp.full((B, H), -1e30, jnp.float32))
    _, h = lax.scan(step, carry0, (to_chunks(q, L), to_chunks(k, L), to_chunks(v, L), gates(i_pre), gates(log_f)))
    return from_chunks(h)


def ab_mixer(hn, w_in, conv_w, conv_b, ret_norm, mlstm_norm, i_bias, f_bias, w_out):
    B, S, _ = hn.shape
    f32 = jnp.float32
    splits = np.cumsum(AB_SIZES)[:-1].tolist()
    rq, rk, rv, rg, mq, mk, mv, mo, mi, mf = jnp.split(hn @ w_in, splits, axis=-1)
    pos = jnp.arange(S)
    heads = lambda t, H: t.astype(f32).reshape(B, S, H, -1)
    ret = retention(rotary(heads(rq, RET_HEADS), pos), rotary(heads(rk, RET_HEADS), pos), heads(rv, RET_HEADS))
    ret = head_group_norm(ret, ret_norm) * jax.nn.silu(rg.astype(f32))
    mqk = jax.nn.silu(causal_dwconv(jnp.concatenate([mq, mk], axis=-1), conv_w, conv_b))
    mq, mk = jnp.split(mqk, 2, axis=-1)
    ml = mlstm(heads(mq, MLSTM_HEADS), heads(mk, MLSTM_HEADS), heads(mv, MLSTM_HEADS),
               mi.astype(f32) + i_bias.astype(f32), mf.astype(f32) + f_bias.astype(f32))
    ml = head_group_norm(ml, mlstm_norm) * jax.nn.sigmoid(mo.astype(f32))
    y = jnp.concatenate([ret, ml], axis=-1).astype(hn.dtype)
    return y @ w_out


def ssd_scan(x, dt, A, Bm, Cm):
    B, S, G, J, P = x.shape
    N = Bm.shape[-1]
    L = SSM_CHUNK
    nC = S // L
    chunk = lambda t: jnp.moveaxis(t.reshape((B, nC, L) + t.shape[2:]), 1, 0)
    mask = jnp.tril(jnp.ones((L, L), dtype=bool))

    def step(state, inp):
        xc, ac, bc, cc = inp
        acum = jnp.cumsum(ac, axis=1)
        at = jnp.moveaxis(acum, 1, -1)
        decay = jnp.exp(jnp.where(mask, at[..., :, None] - at[..., None, :], -jnp.inf))
        cb = jnp.einsum('btgn,bsgn->bgts', cc, bc)
        y_diag = jnp.einsum('bgjts,bsgjp->btgjp', cb[:, :, None] * decay, xc)
        y_off = jnp.einsum('btgn,bgjpn->btgjp', cc, state) * jnp.exp(acum)[..., None]
        last = at[..., -1]
        xw = xc * jnp.moveaxis(jnp.exp(last[..., None] - at), -1, 1)[..., None]
        state = state * jnp.exp(last)[..., None, None] + jnp.einsum('bsgn,bsgjp->bgjpn', bc, xw)
        return state, y_diag + y_off

    xdt = x * dt[..., None]
    a = dt * A
    state0 = jnp.zeros((B, G, J, P, N), jnp.float32)
    _, y = lax.scan(step, state0, (chunk(xdt), chunk(a), chunk(Bm), chunk(Cm)))
    return jnp.moveaxis(y, 0, 1).reshape(B, S, G, J, P)


def ssd_mixer(hn, w_in, conv_w, conv_b, dt_bias, a_log, d_skip, norm_w, w_out):
    B, S, _ = hn.shape
    f32 = jnp.float32
    z, xbc, dt = jnp.split(hn @ w_in, [SSM_INNER, SSM_INNER + SSM_CONV_DIM], axis=-1)
    xbc = jax.nn.silu(causal_dwconv(xbc, conv_w, conv_b)).astype(f32)
    xs, Bm, Cm = jnp.split(xbc, [SSM_INNER, SSM_INNER + SSM_GROUPS * SSM_STATE], axis=-1)
    G, J, P, N = SSM_GROUPS, SSM_HEADS_PER_GROUP, SSM_HEAD_DIM, SSM_STATE
    dt = jax.nn.softplus(dt.astype(f32) + dt_bias.astype(f32)).reshape(B, S, G, J)
    A = -jnp.exp(a_log.astype(f32)).reshape(G, J)
    xs = xs.reshape(B, S, G, J, P)
    y = ssd_scan(xs, dt, A, Bm.reshape(B, S, G, N), Cm.reshape(B, S, G, N))
    y = y + xs * d_skip.astype(f32).reshape(G, J, 1)
    y = y.reshape(B, S, G, J * P) * jax.nn.silu(z.astype(f32)).reshape(B, S, G, J * P)
    y = y * lax.rsqrt(jnp.mean(y * y, axis=-1, keepdims=True) + EPS)
    y = (y.reshape(B, S, SSM_INNER) * norm_w.astype(f32)).astype(hn.dtype)
    return y @ w_out


def peer(hn, w_q, sub_keys, u, v):
    B, S, Dm = hn.shape
    T = B * S
    K = PEER_TOPK
    hf = hn.reshape(T, Dm)
    q = (hf @ w_q).astype(jnp.float32).reshape(T, PEER_HEADS, 2, PEER_DKEY // 2)
    scores = jnp.einsum('thcd,hckd->thck', q, sub_keys.astype(jnp.float32))
    top_s, top_i = lax.top_k(scores, K)
    cand_s = (top_s[:, :, 0, :, None] + top_s[:, :, 1, None, :]).reshape(T, PEER_HEADS, K * K)
    cand_i = (top_i[:, :, 0, :, None] * PEER_NKEYS + top_i[:, :, 1, None, :]).reshape(T, PEER_HEADS, K * K)
    best_s, best_pos = lax.top_k(cand_s, K)
    expert_idx = jnp.take_along_axis(cand_i, best_pos, axis=-1)
    gate = jax.nn.softmax(best_s, axis=-1)
    nb = T // PEER_TOKEN_BLOCK
    E = PEER_HEADS * K
    xb = hf.reshape(nb, PEER_TOKEN_BLOCK, Dm)
    ib = expert_idx.reshape(nb, PEER_TOKEN_BLOCK, E)
    gb = gate.reshape(nb, PEER_TOKEN_BLOCK, E).astype(hn.dtype)

    def block(args):
        xt, it, gt = args
        a = jnp.einsum('td,ted->te', xt, jnp.take(u, it, axis=0))
        w = jax.nn.gelu(a, approximate=False) * gt
        return jnp.einsum('te,ted->td', w, jnp.take(v, it, axis=0))

    out = lax.map(block, (xb, ib, gb))
    return out.reshape(B, S, Dm).astype(hn.dtype)


def setup_inputs(seed: int = 0) -> dict:
    key = jax.random.key(seed)
    ks = jax.random.split(key, 32)
    f32 = jnp.float32
    nrm = lambda k, shape, scale: jax.random.normal(k, shape, f32) * scale
    gain = lambda k, shape: 1.0 + 0.02 * jax.random.normal(k, shape, f32)
    dt0 = jnp.exp(jax.random.uniform(ks[13], (N_ODD, SSM_HEADS), f32, math.log(1e-3), math.log(1e-1)))
    return {
        'x': nrm(ks[0], (BATCH, SEQ, D_MODEL), 1.0),
        'p': nrm(ks[1], (DEPTH, BATCH, SEQ, PLE_DIM), 1.0),
        'ab_w_in': nrm(ks[2], (N_EVEN, D_MODEL, AB_IN), D_MODEL ** -0.5),
        'ab_conv_w': nrm(ks[3], (N_EVEN, SHORT_CONV, 2 * MLSTM_WIDTH), SHORT_CONV ** -0.5),
        'ab_conv_b': nrm(ks[4], (N_EVEN, 2 * MLSTM_WIDTH), 0.01),
        'ab_ret_norm': gain(ks[5], (N_EVEN, RET_WIDTH)),
        'ab_mlstm_norm': gain(ks[6], (N_EVEN, MLSTM_WIDTH)),
        'ab_i_bias': nrm(ks[7], (N_EVEN, MLSTM_HEADS), 0.1),
        'ab_f_bias': jax.random.uniform(ks[8], (N_EVEN, MLSTM_HEADS), f32, 3.0, 6.0),
        'ab_w_out': nrm(ks[9], (N_EVEN, RET_WIDTH + MLSTM_WIDTH, D_MODEL), (RET_WIDTH + MLSTM_WIDTH) ** -0.5),
        'ssd_w_in': nrm(ks[10], (N_ODD, D_MODEL, SSM_IN), D_MODEL ** -0.5),
        'ssd_conv_w': nrm(ks[11], (N_ODD, SHORT_CONV, SSM_CONV_DIM), SHORT_CONV ** -0.5),
        'ssd_conv_b': nrm(ks[12], (N_ODD, SSM_CONV_DIM), 0.01),
        'ssd_dt_bias': dt0 + jnp.log(-jnp.expm1(-dt0)),
        'ssd_a_log': jnp.log(jax.random.uniform(ks[14], (N_ODD, SSM_HEADS), f32, 1.0, 16.0)),
        'ssd_d': gain(ks[15], (N_ODD, SSM_HEADS)),
        'ssd_norm': gain(ks[16], (N_ODD, SSM_INNER)),
        'ssd_w_out': nrm(ks[17], (N_ODD, SSM_INNER, D_MODEL), SSM_INNER ** -0.5),
        'norm_mix': gain(ks[18], (DEPTH, D_MODEL)),
        'norm_ffn': gain(ks[19], (DEPTH, D_MODEL)),
        'peer_w_q': nrm(ks[20], (DEPTH, D_MODEL, PEER_HEADS * PEER_DKEY), D_MODEL ** -0.5),
        'peer_keys': nrm(ks[21], (DEPTH, PEER_HEADS, 2, PEER_NKEYS, PEER_DKEY // 2), (PEER_DKEY // 2) ** -0.5),
        'peer_u': nrm(ks[22], (DEPTH, PEER_EXPERTS, D_MODEL), D_MODEL ** -0.5),
        'peer_v': nrm(ks[23], (DEPTH, PEER_EXPERTS, D_MODEL), PEER_HEADS ** -0.5),
        'ple_norm': gain(ks[24], (DEPTH, D_MODEL)),
        'ple_gate_w': nrm(ks[25], (DEPTH, D_MODEL, D_MODEL), D_MODEL ** -0.5),
        'ple_w': nrm(ks[26], (DEPTH, PLE_DIM, D_MODEL), PLE_DIM ** -0.5),
        'final_norm': gain(ks[27], (D_MODEL,)),
    }


def reference(x, p, ab_w_in, ab_conv_w, ab_conv_b, ab_ret_norm, ab_mlstm_norm, ab_i_bias, ab_f_bias, ab_w_out,
              ssd_w_in, ssd_conv_w, ssd_conv_b, ssd_dt_bias, ssd_a_log, ssd_d, ssd_norm, ssd_w_out,
              norm_mix, norm_ffn, peer_w_q, peer_keys, peer_u, peer_v, ple_norm, ple_gate_w, ple_w, final_norm):
    h = x
    for layer in range(DEPTH):
        j = layer // 2
        hn = rms_norm(h, norm_mix[layer])
        if layer % 2 == 0:
            mix = ab_mixer(hn, ab_w_in[j], ab_conv_w[j], ab_conv_b[j], ab_ret_norm[j], ab_mlstm_norm[j],
                           ab_i_bias[j], ab_f_bias[j], ab_w_out[j])
        else:
            mix = ssd_mixer(hn, ssd_w_in[j], ssd_conv_w[j], ssd_conv_b[j], ssd_dt_bias[j], ssd_a_log[j],
                            ssd_d[j], ssd_norm[j], ssd_w_out[j])
        h = h + mix.astype(h.dtype)
        h = h + peer(rms_norm(h, norm_ffn[layer]), peer_w_q[layer], peer_keys[layer], peer_u[layer], peer_v[layer]).astype(h.dtype)
        gate = jax.nn.sigmoid((rms_norm(h, ple_norm[layer]) @ ple_gate_w[layer]).astype(jnp.float32))
        h = h + ((p[layer] @ ple_w[layer]).astype(jnp.float32) * gate).astype(h.dtype)
    return rms_norm(h, final_norm)
```

```python
import functools
import math

import jax
import jax.numpy as jnp
from jax import lax
from jax.experimental import pallas as pl
from jax.experimental.pallas import tpu as pltpu

f32 = jnp.float32
bf16 = jnp.bfloat16

EPS = 1e-6
ROPE_BASE = 10000.0
CHUNK = 128
CONV_K = 4
CONV_PAD = 8
RET_HEADS = 4
MLSTM_HEADS = 4
HEAD_DIM = 256
SSM_GROUPS = 8
SSM_HPG = 8
SSM_P = 64
SSM_N = 128
SSM_GW = SSM_HPG * SSM_P
PEER_HEADS = 8
PEER_NKEYS = 128
PEER_TOPK = 16
LANES = 128
MIB = 1024 * 1024
HI = lax.Precision.HIGHEST

_TN = (((1,), (1,)), ((), ()))
_TM = (((0,), (0,)), ((), ()))


def _cparams(sem, vmem_mib):
    return pltpu.CompilerParams(dimension_semantics=sem, vmem_limit_bytes=vmem_mib * MIB)


def _silu(x):
    return x * jax.nn.sigmoid(x)


def _rms(x, gain):
    return x * lax.rsqrt(jnp.mean(x * x, axis=-1, keepdims=True) + EPS) * gain


def _group_norm(y, gain):
    mu = jnp.mean(y, axis=-1, keepdims=True)
    yc = y - mu
    return yc * lax.rsqrt(jnp.mean(yc * yc, axis=-1, keepdims=True) + EPS) * gain


def _tril_mask(n):
    t = lax.broadcasted_iota(jnp.int32, (n, n), 0)
    s = lax.broadcasted_iota(jnp.int32, (n, n), 1)
    return s <= t


def _triu_mask(n):
    t = lax.broadcasted_iota(jnp.int32, (n, n), 0)
    s = lax.broadcasted_iota(jnp.int32, (n, n), 1)
    return t <= s


def _causal_conv(x, hist_ref, w, b):
    n = x.shape[0]
    hist_ref[CONV_PAD:CONV_PAD + n, :] = x
    y = b + w[CONV_K - 1:CONV_K, :] * x
    for j in range(1, CONV_K):
        y = y + w[CONV_K - 1 - j:CONV_K - j, :] * hist_ref[pl.ds(CONV_PAD - j, n), :]
    hist_ref[0:CONV_PAD, :] = hist_ref[n:n + CONV_PAD, :]
    return y


def _norm_matmul_kernel(*refs, has_aux, emit_xn):
    x_ref, g_ref, w_ref = refs[:3]
    pos = 3
    waux_ref = None
    if has_aux:
        waux_ref = refs[pos]
        pos += 1
    o_ref = refs[pos]
    pos += 1
    aux_ref = xn_out_ref = None
    if has_aux:
        aux_ref = refs[pos]
        pos += 1
    if emit_xn:
        xn_out_ref = refs[pos]
        pos += 1
    xn_ref = refs[pos]

    @pl.when(pl.program_id(1) == 0)
    def _():
        xn = _rms(x_ref[...], g_ref[...]).astype(bf16)
        xn_ref[...] = xn
        if has_aux:
            aux_ref[...] = jnp.dot(xn, waux_ref[...], preferred_element_type=f32)
        if emit_xn:
            xn_out_ref[...] = xn

    o_ref[...] = jnp.dot(xn_ref[...], w_ref[...], preferred_element_type=f32).astype(o_ref.dtype)


def _norm_matmul(x, gain, w, w_aux=None, *, out_dtype, emit_xn=False, tm=1024, tn=1024):
    T, D = x.shape
    N = w.shape[1]
    tm = min(tm, T)
    tn = min(tn, N)
    has_aux = w_aux is not None
    in_specs = [pl.BlockSpec((tm, D), lambda i, j: (i, 0)),
                pl.BlockSpec((1, D), lambda i, j: (0, 0)),
                pl.BlockSpec((D, tn), lambda i, j: (0, j))]
    args = [x, gain.reshape(1, D), w]
    out_shape = [jax.ShapeDtypeStruct((T, N), out_dtype)]
    out_specs = [pl.BlockSpec((tm, tn), lambda i, j: (i, j))]
    if has_aux:
        in_specs.append(pl.BlockSpec((D, LANES), lambda i, j: (0, 0)))
        args.append(w_aux)
        out_shape.append(jax.ShapeDtypeStruct((T, LANES), f32))
        out_specs.append(pl.BlockSpec((tm, LANES), lambda i, j: (i, 0)))
    if emit_xn:
        out_shape.append(jax.ShapeDtypeStruct((T, D), bf16))
        out_specs.append(pl.BlockSpec((tm, D), lambda i, j: (i, 0)))
    return pl.pallas_call(
        functools.partial(_norm_matmul_kernel, has_aux=has_aux, emit_xn=emit_xn),
        grid=(T // tm, N // tn),
        in_specs=in_specs, out_specs=out_specs, out_shape=out_shape,
        scratch_shapes=[pltpu.VMEM((tm, D), bf16)],
        compiler_params=_cparams(("parallel", "arbitrary"), 48),
        name="norm_matmul",
    )(*args)


def _matmul_res_kernel(*refs, n_in):
    a_refs = refs[:n_in]
    w_refs = refs[n_in:2 * n_in]
    r_ref = refs[2 * n_in]
    o_ref = refs[2 * n_in + 1]
    acc = r_ref[...]
    for a_ref, w_ref in zip(a_refs, w_refs):
        acc = acc + jnp.dot(a_ref[...], w_ref[...], preferred_element_type=f32)
    o_ref[...] = acc


def _matmul_res(a_list, w_list, res, *, tm=1024, tn=512):
    T, N = res.shape
    tm = min(tm, T)
    tn = min(tn, N)
    n_in = len(a_list)
    in_specs = ([pl.BlockSpec((tm, a.shape[1]), lambda i, j: (i, 0)) for a in a_list]
                + [pl.BlockSpec((w.shape[0], tn), lambda i, j: (0, j)) for w in w_list]
                + [pl.BlockSpec((tm, tn), lambda i, j: (i, j))])
    return pl.pallas_call(
        functools.partial(_matmul_res_kernel, n_in=n_in),
        grid=(T // tm, N // tn),
        in_specs=in_specs,
        out_specs=pl.BlockSpec((tm, tn), lambda i, j: (i, j)),
        out_shape=jax.ShapeDtypeStruct((T, N), f32),
        compiler_params=_cparams(("parallel", "parallel"), 48),
        name="matmul_res",
    )(*a_list, *w_list, res)


def _retention_kernel(q_ref, k_ref, v_ref, g_ref, cos_ref, sin_ref, lg_ref, nw_ref, o_ref, r_ref):
    @pl.when(pl.program_id(2) == 0)
    def _():
        r_ref[...] = jnp.zeros_like(r_ref)

    n = q_ref.shape[1]
    half = HEAD_DIM // 2
    cos = cos_ref[...]
    sin = sin_ref[...]

    def rot(t):
        t1, t2 = t[:, :half], t[:, half:]
        return jnp.concatenate([t1 * cos - t2 * sin, t1 * sin + t2 * cos], axis=-1)

    q = rot(q_ref[0].astype(f32))
    k = rot(k_ref[0].astype(f32)) * (HEAD_DIM ** -0.5)
    v = v_ref[0]
    lg = lg_ref[0]
    lg1 = lg[:, :1]
    ti = lax.broadcasted_iota(jnp.int32, (n, n), 0)
    si = lax.broadcasted_iota(jnp.int32, (n, n), 1)
    rel = (ti - si).astype(f32)
    decay = jnp.where(rel >= 0, jnp.exp(rel * lg), 0.0)
    idx = lax.broadcasted_iota(jnp.int32, (n, 1), 0).astype(f32)
    q_decay = jnp.exp((idx + 1.0) * lg1)
    k_decay = jnp.exp((n - 1.0 - idx) * lg1)
    chunk_decay = jnp.exp(float(n) * lg1)
    qb = q.astype(bf16)
    inner = lax.dot_general(qb, k.astype(bf16), _TN, preferred_element_type=f32) * decay
    r = r_ref[...]
    o = (jnp.dot(inner.astype(bf16), v, preferred_element_type=f32)
         + jnp.dot(qb, r.astype(bf16), preferred_element_type=f32) * q_decay)
    r_ref[...] = r * chunk_decay + lax.dot_general((k * k_decay).astype(bf16), v, _TM,
                                                   preferred_element_type=f32)
    g = g_ref[0].astype(f32)
    o_ref[0] = (_group_norm(o, nw_ref[...]) * _silu(g)).astype(o_ref.dtype)


def _retention(proj, cos, sin, log_g, norm_w):
    B, S, _ = proj.shape
    nC = S // CHUNK
    H = RET_HEADS
    blk = lambda off: pl.BlockSpec((1, CHUNK, HEAD_DIM), lambda b, h, c: (b, c, off + h))
    return pl.pallas_call(
        _retention_kernel,
        grid=(B, H, nC),
        in_specs=[blk(0), blk(H), blk(2 * H), blk(3 * H),
                  pl.BlockSpec((CHUNK, HEAD_DIM // 2), lambda b, h, c: (c, 0)),
                  pl.BlockSpec((CHUNK, HEAD_DIM // 2), lambda b, h, c: (c, 0)),
                  pl.BlockSpec((1, 1, CHUNK), lambda b, h, c: (h, 0, 0)),
                  pl.BlockSpec((1, HEAD_DIM), lambda b, h, c: (0, h))],
        out_specs=pl.BlockSpec((1, CHUNK, HEAD_DIM), lambda b, h, c: (b, c, h)),
        out_shape=jax.ShapeDtypeStruct((B, S, H * HEAD_DIM), bf16),
        scratch_shapes=[pltpu.VMEM((HEAD_DIM, HEAD_DIM), f32)],
        compiler_params=_cparams(("parallel", "parallel", "arbitrary"), 32),
        name="retention",
    )(proj, proj, proj, proj, cos, sin, log_g, norm_w)


def _mlstm_kernel(q_ref, k_ref, v_ref, og_ref, gc_ref, gr_ref, bc_ref, br_ref,
                  cwq_ref, cwk_ref, cbq_ref, cbk_ref, nw_ref, o_ref,
                  c_ref, n_ref, m_ref, hq_ref, hk_ref):
    @pl.when(pl.program_id(2) == 0)
    def _():
        c_ref[...] = jnp.zeros_like(c_ref)
        n_ref[...] = jnp.zeros_like(n_ref)
        m_ref[...] = jnp.full(m_ref.shape, -1e30, f32)
        hq_ref[0:CONV_PAD, :] = jnp.zeros((CONV_PAD, HEAD_DIM), f32)
        hk_ref[0:CONV_PAD, :] = jnp.zeros((CONV_PAD, HEAD_DIM), f32)

    n = q_ref.shape[1]
    q = _silu(_causal_conv(q_ref[0].astype(f32), hq_ref, cwq_ref[...], cbq_ref[...]))
    k = _silu(_causal_conv(k_ref[0].astype(f32), hk_ref, cwk_ref[...], cbk_ref[...])) * (HEAD_DIM ** -0.5)
    v = v_ref[0]
    gc = gc_ref[0, 0] + bc_ref[0]
    gr = gr_ref[0, 0] + br_ref[0]
    i_c = gc[:, 0:1]
    lf_c = jax.nn.log_sigmoid(gc[:, 1:2])
    i_r = gr[0:1, :]
    lf_r = jax.nn.log_sigmoid(gr[1:2, :])
    tril = _tril_mask(n)
    b_c = jnp.sum(jnp.where(tril, lf_r, 0.0), axis=-1, keepdims=True)
    b_r = jnp.sum(jnp.where(_triu_mask(n), lf_c, 0.0), axis=0, keepdims=True)
    m_prev = m_ref[...]
    log_intra = jnp.where(tril, b_c - b_r + i_r, -jnp.inf)
    log_inter = b_c + m_prev
    m_t = jnp.maximum(log_inter, jnp.max(log_intra, axis=-1, keepdims=True))
    w_intra = jnp.exp(log_intra - m_t)
    w_inter = jnp.exp(log_inter - m_t)
    qb = q.astype(bf16)
    s = lax.dot_general(qb, k.astype(bf16), _TN, preferred_element_type=f32) * w_intra
    c_st = c_ref[...]
    n_st = n_ref[...]
    num = (jnp.dot(s.astype(bf16), v, preferred_element_type=f32)
           + w_inter * jnp.dot(qb, c_st.astype(bf16), preferred_element_type=f32))
    den = (jnp.sum(s, axis=-1, keepdims=True)
           + w_inter * jnp.sum(qb.astype(f32) * n_st, axis=-1, keepdims=True))
    hh = num / jnp.maximum(jnp.abs(den), jnp.exp(-m_t))
    m_new = m_t[n - 1:n, :]
    b_last = b_c[n - 1:n, :]
    w_end = jnp.exp(b_last - b_c + i_c - m_new)
    w_prev = jnp.exp(b_last + m_prev - m_new)
    kw = k * w_end
    c_ref[...] = w_prev * c_st + lax.dot_general(kw.astype(bf16), v, _TM, preferred_element_type=f32)
    n_ref[...] = w_prev * n_st + jnp.sum(kw, axis=0, keepdims=True)
    m_ref[...] = m_new
    og = og_ref[0].astype(f32)
    o_ref[0] = (_group_norm(hh, nw_ref[...]) * jax.nn.sigmoid(og)).astype(o_ref.dtype)


def _mlstm(proj, gates_c, gates_r, bias_c, bias_r, conv_w, conv_b, norm_w):
    B, S, _ = proj.shape
    nC = S // CHUNK
    H = MLSTM_HEADS
    base = 4 * RET_HEADS
    blk = lambda off: pl.BlockSpec((1, CHUNK, HEAD_DIM), lambda b, h, c: (b, c, base + off + h))
    return pl.pallas_call(
        _mlstm_kernel,
        grid=(B, H, nC),
        in_specs=[blk(0), blk(H), blk(2 * H), blk(3 * H),
                  pl.BlockSpec((1, 1, CHUNK, 2), lambda b, h, c: (b, h, c, 0)),
                  pl.BlockSpec((1, 1, 2, CHUNK), lambda b, h, c: (b, h, 0, c)),
                  pl.BlockSpec((1, 1, 2), lambda b, h, c: (h, 0, 0)),
                  pl.BlockSpec((1, 2, 1), lambda b, h, c: (h, 0, 0)),
                  pl.BlockSpec((CONV_K, HEAD_DIM), lambda b, h, c: (0, h)),
                  pl.BlockSpec((CONV_K, HEAD_DIM), lambda b, h, c: (0, H + h)),
                  pl.BlockSpec((1, HEAD_DIM), lambda b, h, c: (0, h)),
                  pl.BlockSpec((1, HEAD_DIM), lambda b, h, c: (0, H + h)),
                  pl.BlockSpec((1, HEAD_DIM), lambda b, h, c: (0, h))],
        out_specs=pl.BlockSpec((1, CHUNK, HEAD_DIM), lambda b, h, c: (b, c, h)),
        out_shape=jax.ShapeDtypeStruct((B, S, H * HEAD_DIM), bf16),
        scratch_shapes=[pltpu.VMEM((HEAD_DIM, HEAD_DIM), f32),
                        pltpu.VMEM((1, HEAD_DIM), f32),
                        pltpu.VMEM((1, 1), f32),
                        pltpu.VMEM((CHUNK + CONV_PAD, HEAD_DIM), f32),
                        pltpu.VMEM((CHUNK + CONV_PAD, HEAD_DIM), f32)],
        compiler_params=_cparams(("parallel", "parallel", "arbitrary"), 32),
        name="mlstm",
    )(proj, proj, proj, proj, gates_c, gates_r, bias_c, bias_r, conv_w, conv_w, conv_b, conv_b, norm_w)


def _ssd_kernel(z_ref, x_ref, b_ref, c_ref, dtc_ref, dtr_ref,
                cwx_ref, cwb_ref, cwc_ref, cbx_ref, cbb_ref, cbc_ref,
                dbr_ref, dbc_ref, alr_ref, alc_ref, dsk_ref, nw_ref, ex_ref, o_ref,
                st_ref, hx_ref, hb_ref, hc_ref):
    @pl.when(pl.program_id(2) == 0)
    def _():
        st_ref[...] = jnp.zeros_like(st_ref)
        hx_ref[0:CONV_PAD, :] = jnp.zeros((CONV_PAD, SSM_GW), f32)
        hb_ref[0:CONV_PAD, :] = jnp.zeros((CONV_PAD, SSM_N), f32)
        hc_ref[0:CONV_PAD, :] = jnp.zeros((CONV_PAD, SSM_N), f32)

    n = x_ref.shape[1]
    x = _silu(_causal_conv(x_ref[0].astype(f32), hx_ref, cwx_ref[...], cbx_ref[...]))
    bm = _silu(_causal_conv(b_ref[0].astype(f32), hb_ref, cwb_ref[...], cbb_ref[...]))
    cm = _silu(_causal_conv(c_ref[0].astype(f32), hc_ref, cwc_ref[...], cbc_ref[...]))
    bmb = bm.astype(bf16)
    cmb = cm.astype(bf16)

    dt_c = jax.nn.softplus(dtc_ref[0, 0] + dbr_ref[0])
    a_c = dt_c * (-jnp.exp(alr_ref[0]))
    dt_r = jax.nn.softplus(dtr_ref[0, 0] + dbc_ref[0])
    a_r = dt_r * (-jnp.exp(alc_ref[0]))
    tril = _tril_mask(n)
    lower = tril.astype(f32)
    upper = _triu_mask(n).astype(f32)
    acum_c = jnp.dot(lower, a_c, precision=HI, preferred_element_type=f32)
    acum_r = jnp.dot(a_r, upper, precision=HI, preferred_element_type=f32)
    last = acum_c[n - 1:n, :]
    ex = ex_ref[...]
    expand = lambda t: jnp.dot(t, ex, precision=HI, preferred_element_type=f32)
    full = expand(jnp.concatenate([dt_c, jnp.exp(acum_c), jnp.exp(last - acum_c),
                                   jnp.broadcast_to(jnp.exp(last), (8, SSM_HPG))], axis=0))
    dt_f = full[0:n]
    ea_f = full[n:2 * n]
    ew_f = full[2 * n:3 * n]
    el_f = full[3 * n:3 * n + 1]
    xdt = x * dt_f

    cb = lax.dot_general(cmb, bmb, _TN, preferred_element_type=f32)
    lane = lax.broadcasted_iota(jnp.int32, (n, LANES), 1)
    ys = []
    for pair in range(SSM_GW // LANES):
        xp = xdt[:, pair * LANES:(pair + 1) * LANES]
        acc = None
        for sub in range(LANES // SSM_P):
            jh = pair * (LANES // SSM_P) + sub
            dec = jnp.exp(jnp.where(tril, acum_c[:, jh:jh + 1] - acum_r[jh:jh + 1, :], -jnp.inf))
            sel = (lane >= sub * SSM_P) & (lane < (sub + 1) * SSM_P)
            y = jnp.dot((cb * dec).astype(bf16), jnp.where(sel, xp, 0.0).astype(bf16),
                        preferred_element_type=f32)
            acc = y if acc is None else acc + y
        ys.append(acc)
    y_diag = jnp.concatenate(ys, axis=-1)

    st = st_ref[...]
    y_off = jnp.dot(cmb, st.astype(bf16), preferred_element_type=f32) * ea_f
    xw = xdt * ew_f
    st_ref[...] = st * el_f + lax.dot_general(bmb, xw.astype(bf16), _TM, preferred_element_type=f32)

    y = y_diag + y_off + x * dsk_ref[...]
    y = y * _silu(z_ref[0].astype(f32))
    o_ref[0] = _rms(y, nw_ref[...]).astype(o_ref.dtype)


def _ssd(proj, dt_c, dt_r, conv_w, conv_b, dtb_r, dtb_c, al_r, al_c, dskip, norm_w, expand):
    B, S, _ = proj.shape
    nC = S // CHUNK
    G = SSM_GROUPS
    inner = G * SSM_GW
    xo = inner // SSM_GW
    bo = 2 * inner // SSM_N
    co = bo + G
    cw_bo = inner // SSM_N
    cw_co = cw_bo + G
    im3 = lambda off: (lambda b, g, c: (b, c, off + g))
    im2 = lambda off: (lambda b, g, c: (0, off + g))
    gvec = lambda shape: pl.BlockSpec((1,) + shape, lambda b, g, c: (g, 0, 0))
    return pl.pallas_call(
        _ssd_kernel,
        grid=(B, G, nC),
        in_specs=[pl.BlockSpec((1, CHUNK, SSM_GW), im3(0)),
                  pl.BlockSpec((1, CHUNK, SSM_GW), im3(xo)),
                  pl.BlockSpec((1, CHUNK, SSM_N), im3(bo)),
                  pl.BlockSpec((1, CHUNK, SSM_N), im3(co)),
                  pl.BlockSpec((1, 1, CHUNK, SSM_HPG), lambda b, g, c: (b, g, c, 0)),
                  pl.BlockSpec((1, 1, SSM_HPG, CHUNK), lambda b, g, c: (b, g, 0, c)),
                  pl.BlockSpec((CONV_K, SSM_GW), im2(0)),
                  pl.BlockSpec((CONV_K, SSM_N), im2(cw_bo)),
                  pl.BlockSpec((CONV_K, SSM_N), im2(cw_co)),
                  pl.BlockSpec((1, SSM_GW), im2(0)),
                  pl.BlockSpec((1, SSM_N), im2(cw_bo)),
                  pl.BlockSpec((1, SSM_N), im2(cw_co)),
                  gvec((1, SSM_HPG)), gvec((SSM_HPG, 1)), gvec((1, SSM_HPG)), gvec((SSM_HPG, 1)),
                  pl.BlockSpec((1, SSM_GW), im2(0)),
                  pl.BlockSpec((1, SSM_GW), im2(0)),
                  pl.BlockSpec((SSM_HPG, SSM_GW), lambda b, g, c: (0, 0))],
        out_specs=pl.BlockSpec((1, CHUNK, SSM_GW), lambda b, g, c: (b, c, g)),
        out_shape=jax.ShapeDtypeStruct((B, S, inner), bf16),
        scratch_shapes=[pltpu.VMEM((SSM_N, SSM_GW), f32),
                        pltpu.VMEM((CHUNK + CONV_PAD, SSM_GW), f32),
                        pltpu.VMEM((CHUNK + CONV_PAD, SSM_N), f32),
                        pltpu.VMEM((CHUNK + CONV_PAD, SSM_N), f32)],
        compiler_params=_cparams(("parallel", "parallel", "arbitrary"), 32),
        name="ssd",
    )(proj, proj, proj, proj, dt_c, dt_r, conv_w, conv_w, conv_w, conv_b, conv_b, conv_b,
      dtb_r, dtb_c, al_r, al_c, dskip, norm_w, expand)


def _top_values(x, count):
    vals = []
    for _ in range(count):
        m = jnp.max(x, axis=0, keepdims=True)
        vals.append(m)
        x = jnp.where(x == m, -jnp.inf, x)
    return vals


def _peer_route_kernel(q_ref, keys_ref, s2_ref, e2_ref, c1_ref, e1_ref):
    qb = q_ref[...]
    dk = keys_ref.shape[-1]
    s1 = lax.dot_general(keys_ref[0, 0], qb[:, :dk], _TN, preferred_element_type=f32)
    s2 = lax.dot_general(keys_ref[0, 1], qb[:, dk:], _TN, preferred_element_type=f32)
    a = _top_values(s1, PEER_TOPK + 1)
    b = _top_values(s2, PEER_TOPK + 1)
    bmat = jnp.concatenate(b[:PEER_TOPK], axis=0)
    cand = jnp.concatenate([a[r] + bmat for r in range(PEER_TOPK)], axis=0)
    cs = _top_values(cand, PEER_TOPK + 1)
    dropped = jnp.maximum(cs[PEER_TOPK], jnp.maximum(a[PEER_TOPK] + b[0], a[0] + b[PEER_TOPK]))
    thr = 0.5 * (cs[PEER_TOPK - 1] + dropped)
    z = jnp.exp(cs[0] - cs[0])
    for r in range(1, PEER_TOPK):
        z = z + jnp.exp(cs[r] - cs[0])
    s2_ref[0] = s2
    e2_ref[0] = jnp.exp(s2 - b[0])
    c1_ref[0] = thr - s1
    e1_ref[0] = jnp.exp(s1 - a[0]) / z


def _peer_route(q, keys, *, tb=512):
    T = q.shape[0]
    tb = min(tb, T)
    H = PEER_HEADS
    nk = PEER_NKEYS
    dk = keys.shape[-1]
    out = jax.ShapeDtypeStruct((H, nk, T), f32)
    ospec = pl.BlockSpec((1, nk, tb), lambda i, h: (h, 0, i))
    return pl.pallas_call(
        _peer_route_kernel,
        grid=(T // tb, H),
        in_specs=[pl.BlockSpec((tb, 2 * dk), lambda i, h: (i, h)),
                  pl.BlockSpec((1, 2, nk, dk), lambda i, h: (h, 0, 0, 0))],
        out_specs=[ospec] * 4,
        out_shape=[out] * 4,
        compiler_params=_cparams(("parallel", "parallel"), 32),
        name="peer_route",
    )(q, keys)


def _peer_dense_kernel(hn_ref, u_ref, vt_ref, s2_ref, e2_ref, c1_ref, e1_ref, o_ref,
                       acc_ref, at_ref, w_ref, *, n1, tc):
    j = pl.program_id(1)

    @pl.when(j == 0)
    def _():
        acc_ref[...] = jnp.zeros_like(acc_ref)

    at_ref[...] = lax.dot_general(u_ref[...], hn_ref[...], _TN, preferred_element_type=f32)
    tb = hn_ref.shape[0]
    nk = PEER_NKEYS
    for il in range(n1):
        rows = slice(il * nk, (il + 1) * nk)
        for ct in range(tb // tc):
            cols = slice(ct * tc, (ct + 1) * tc)
            a = at_ref[rows, cols]
            g = None
            for h in range(PEER_HEADS):
                keep = s2_ref[h, :, cols] >= c1_ref[h, il:il + 1, cols]
                term = jnp.where(keep, e2_ref[h, :, cols] * e1_ref[h, il:il + 1, cols], 0.0)
                g = term if g is None else g + term
            gelu = 0.5 * a * (1.0 + lax.erf(a * (2.0 ** -0.5)))
            w_ref[rows, cols] = (gelu * g).astype(bf16)
    acc_ref[...] += jnp.dot(vt_ref[...], w_ref[...], preferred_element_type=f32)

    @pl.when(j == pl.num_programs(1) - 1)
    def _():
        o_ref[...] = acc_ref[...].T


def _peer_dense(hn, u, vt, s2, e2, c1, e1, *, tb=512, n1=8):
    T, D = hn.shape
    tb = min(tb, T)
    nk = PEER_NKEYS
    te = n1 * nk
    H = PEER_HEADS
    tc = min(256, tb)
    full = pl.BlockSpec((H, nk, tb), lambda i, j: (0, 0, i))
    part = pl.BlockSpec((H, n1, tb), lambda i, j: (0, j, i))
    return pl.pallas_call(
        functools.partial(_peer_dense_kernel, n1=n1, tc=tc),
        grid=(T // tb, nk // n1),
        in_specs=[pl.BlockSpec((tb, D), lambda i, j: (i, 0)),
                  pl.BlockSpec((te, D), lambda i, j: (j, 0)),
                  pl.BlockSpec((D, te), lambda i, j: (0, j)),
                  full, full, part, part],
        out_specs=pl.BlockSpec((tb, D), lambda i, j: (i, 0)),
        out_shape=jax.ShapeDtypeStruct((T, D), f32),
        scratch_shapes=[pltpu.VMEM((D, tb), f32),
                        pltpu.VMEM((te, tb), f32),
                        pltpu.VMEM((te, tb), bf16)],
        compiler_params=_cparams(("parallel", "arbitrary"), 56),
        name="peer_dense",
    )(hn, u, vt, s2, e2, c1, e1)


def _ple_kernel(h_ref, d_ref, p_ref, g_ref, wg_ref, pw_ref, fn_ref, o_ref, *, final, tn):
    h = h_ref[...] + d_ref[...]
    xn = _rms(h, g_ref[...]).astype(bf16)
    pb = p_ref[...].astype(bf16)
    D = h.shape[1]
    ss = None
    for c in range(D // tn):
        cols = slice(c * tn, (c + 1) * tn)
        gate = jax.nn.sigmoid(jnp.dot(xn, wg_ref[:, cols], preferred_element_type=f32))
        pv = jnp.dot(pb, pw_ref[:, cols], preferred_element_type=f32)
        h2 = h[:, cols] + pv * gate
        o_ref[:, cols] = h2
        if final:
            part = jnp.sum(h2 * h2, axis=-1, keepdims=True)
            ss = part if ss is None else ss + part
    if final:
        o_ref[...] = o_ref[...] * lax.rsqrt(ss * (1.0 / D) + EPS) * fn_ref[...]


def _ple(h, delta, p, gain, wg, pw, final_gain, *, final, tm=256, tn=512):
    T, D = h.shape
    tm = min(tm, T)
    P = p.shape[1]
    row = lambda i: (i, 0)
    fixed = lambda i: (0, 0)
    return pl.pallas_call(
        functools.partial(_ple_kernel, final=final, tn=tn),
        grid=(T // tm,),
        in_specs=[pl.BlockSpec((tm, D), row), pl.BlockSpec((tm, D), row), pl.BlockSpec((tm, P), row),
                  pl.BlockSpec((1, D), fixed), pl.BlockSpec((D, D), fixed), pl.BlockSpec((P, D), fixed),
                  pl.BlockSpec((1, D), fixed)],
        out_specs=pl.BlockSpec((tm, D), row),
        out_shape=jax.ShapeDtypeStruct((T, D), f32),
        compiler_params=_cparams(("parallel",), 48),
        name="ple",
    )(h, delta, p, gain.reshape(1, D), wg, pw, final_gain.reshape(1, D))


def _pad_cols(w, width):
    return jnp.pad(w, ((0, 0), (0, width - w.shape[1])))


def _peer_ffn(h, norm_w, w_q, keys, u, v):
    q, hn = _norm_matmul(h, norm_w, w_q.astype(bf16), out_dtype=bf16, emit_xn=True)
    s2, e2, c1, e1 = _peer_route(q, keys.astype(bf16))
    return _peer_dense(hn, u.astype(bf16), v.T.astype(bf16), s2, e2, c1, e1)


def _ab_layer(h, B, S, norm_w, w_in, conv_w, conv_b, ret_norm, mlstm_norm, i_bias, f_bias, w_out):
    T, D = h.shape
    width = RET_HEADS * HEAD_DIM
    main = 4 * width + 4 * MLSTM_HEADS * HEAD_DIM
    proj, gates = _norm_matmul(h, norm_w, w_in[:, :main].astype(bf16),
                               _pad_cols(w_in[:, main:], LANES).astype(bf16), out_dtype=bf16)
    proj = proj.reshape(B, S, main)
    half = HEAD_DIM // 2
    inv = ROPE_BASE ** (-jnp.arange(half, dtype=f32) / half)
    ang = jnp.arange(S, dtype=f32)[:, None] * inv[None, :]
    log_g = jnp.log1p(-(2.0 ** (-5.0 - jnp.arange(RET_HEADS, dtype=f32))))
    log_g = jnp.broadcast_to(log_g[:, None, None], (RET_HEADS, 1, CHUNK))
    y_ret = _retention(proj, jnp.cos(ang), jnp.sin(ang), log_g, ret_norm.reshape(1, width))

    H = MLSTM_HEADS
    gi = gates[:, :H].reshape(B, S, H)
    gf = gates[:, H:2 * H].reshape(B, S, H)
    g_c = jnp.stack([gi, gf], axis=-1).transpose(0, 2, 1, 3)
    g_r = jnp.stack([gi, gf], axis=-1).transpose(0, 2, 3, 1)
    bias = jnp.stack([i_bias, f_bias], axis=-1)
    y_ml = _mlstm(proj, g_c, g_r, bias.reshape(H, 1, 2), bias.reshape(H, 2, 1),
                  conv_w, conv_b.reshape(1, -1), mlstm_norm.reshape(1, -1))
    wo = w_out.astype(bf16)
    return _matmul_res([y_ret.reshape(T, width), y_ml.reshape(T, -1)], [wo[:width], wo[width:]], h)


def _ssd_layer(h, B, S, norm_w, w_in, conv_w, conv_b, dt_bias, a_log, d_skip, norm_g, w_out):
    T, D = h.shape
    G, J = SSM_GROUPS, SSM_HPG
    inner = G * SSM_GW
    main = 2 * inner + 2 * G * SSM_N
    proj, dt = _norm_matmul(h, norm_w, w_in[:, :main].astype(bf16),
                            _pad_cols(w_in[:, main:], LANES).astype(bf16), out_dtype=bf16)
    proj = proj.reshape(B, S, main)
    dt = dt[:, :G * J].reshape(B, S, G, J)
    expand = (jnp.arange(SSM_GW)[None, :] // SSM_P == jnp.arange(J)[:, None]).astype(f32)
    y = _ssd(proj, dt.transpose(0, 2, 1, 3), dt.transpose(0, 2, 3, 1), conv_w, conv_b.reshape(1, -1),
             dt_bias.reshape(G, 1, J), dt_bias.reshape(G, J, 1), a_log.reshape(G, 1, J), a_log.reshape(G, J, 1),
             jnp.repeat(d_skip, SSM_P).reshape(1, inner), norm_g.reshape(1, inner), expand)
    return _matmul_res([y.reshape(T, inner)], [w_out.astype(bf16)], h)


def kernel(x, p, ab_w_in, ab_conv_w, ab_conv_b, ab_ret_norm, ab_mlstm_norm, ab_i_bias, ab_f_bias, ab_w_out, ssd_w_in, ssd_conv_w, ssd_conv_b, ssd_dt_bias, ssd_a_log, ssd_d, ssd_norm, ssd_w_out, norm_mix, norm_ffn, peer_w_q, peer_keys, peer_u, peer_v, ple_norm, ple_gate_w, ple_w, final_norm):
    B, S, D = x.shape
    T = B * S
    depth = p.shape[0]
    h = x.reshape(T, D)
    for layer in range(depth):
        j = layer // 2
        if layer % 2 == 0:
            h = _ab_layer(h, B, S, norm_mix[layer], ab_w_in[j], ab_conv_w[j], ab_conv_b[j], ab_ret_norm[j],
                          ab_mlstm_norm[j], ab_i_bias[j], ab_f_bias[j], ab_w_out[j])
        else:
            h = _ssd_layer(h, B, S, norm_mix[layer], ssd_w_in[j], ssd_conv_w[j], ssd_conv_b[j], ssd_dt_bias[j],
                           ssd_a_log[j], ssd_d[j], ssd_norm[j], ssd_w_out[j])
        delta = _peer_ffn(h, norm_ffn[layer], peer_w_q[layer], peer_keys[layer], peer_u[layer], peer_v[layer])
        h = _ple(h, delta, p[layer].reshape(T, -1), ple_norm[layer], ple_gate_w[layer].astype(bf16),
                 ple_w[layer].astype(bf16), final_norm, final=(layer == depth - 1))
    return h.reshape(B, S, D)
```

```python
import functools
import math

import jax
import jax.numpy as jnp
from jax import lax
from jax.experimental import pallas as pl
from jax.experimental.pallas import tpu as pltpu

f32 = jnp.float32
bf16 = jnp.bfloat16

EPS = 1e-6
ROPE_BASE = 10000.0
CHUNK = 128
CONV_K = 4
CONV_PAD = 8
RET_HEADS = 4
MLSTM_HEADS = 4
HEAD_DIM = 256
SSM_GROUPS = 8
SSM_HPG = 8
SSM_P = 64
SSM_N = 128
SSM_GW = SSM_HPG * SSM_P
PEER_HEADS = 8
PEER_NKEYS = 128
PEER_TOPK = 16
LANES = 128
MIB = 1024 * 1024

_TN = (((1,), (1,)), ((), ()))
_TM = (((0,), (0,)), ((), ()))


def _cparams(sem, vmem_mib):
    return pltpu.CompilerParams(dimension_semantics=sem, vmem_limit_bytes=vmem_mib * MIB)


def _silu(x):
    return x * jax.nn.sigmoid(x)


def _rms(x, gain):
    return x * lax.rsqrt(jnp.mean(x * x, axis=-1, keepdims=True) + EPS) * gain


def _group_norm(y, gain):
    mu = jnp.mean(y, axis=-1, keepdims=True)
    yc = y - mu
    return yc * lax.rsqrt(jnp.mean(yc * yc, axis=-1, keepdims=True) + EPS) * gain


def _tril_mask(n):
    t = lax.broadcasted_iota(jnp.int32, (n, n), 0)
    s = lax.broadcasted_iota(jnp.int32, (n, n), 1)
    return s <= t


def _triu_mask(n):
    t = lax.broadcasted_iota(jnp.int32, (n, n), 0)
    s = lax.broadcasted_iota(jnp.int32, (n, n), 1)
    return t <= s


def _split_dot(a, b, terms, split_rhs=False):
    rem = b if split_rhs else a
    out = None
    for _ in range(terms):
        piece = rem.astype(bf16)
        d = jnp.dot(a, piece, preferred_element_type=f32) if split_rhs else jnp.dot(piece, b, preferred_element_type=f32)
        out = d if out is None else out + d
        rem = rem - piece.astype(f32)
    return out


def _causal_conv(x, hist_ref, w, b):
    n = x.shape[0]
    hist_ref[CONV_PAD:CONV_PAD + n, :] = x
    y = b + w[CONV_K - 1:CONV_K, :] * x
    for j in range(1, CONV_K):
        y = y + w[CONV_K - 1 - j:CONV_K - j, :] * hist_ref[pl.ds(CONV_PAD - j, n), :]
    hist_ref[0:CONV_PAD, :] = hist_ref[n:n + CONV_PAD, :]
    return y


def _norm_matmul_kernel(*refs, has_aux, emit_xn):
    x_ref, g_ref, w_ref = refs[:3]
    pos = 3
    waux_ref = None
    if has_aux:
        waux_ref = refs[pos]
        pos += 1
    o_ref = refs[pos]
    pos += 1
    aux_ref = xn_out_ref = None
    if has_aux:
        aux_ref = refs[pos]
        pos += 1
    if emit_xn:
        xn_out_ref = refs[pos]
        pos += 1
    xn_ref = refs[pos]

    @pl.when(pl.program_id(1) == 0)
    def _():
        xn = _rms(x_ref[...], g_ref[...]).astype(bf16)
        xn_ref[...] = xn
        if has_aux:
            aux_ref[...] = jnp.dot(xn, waux_ref[...], preferred_element_type=f32)
        if emit_xn:
            xn_out_ref[...] = xn

    o_ref[...] = jnp.dot(xn_ref[...], w_ref[...], preferred_element_type=f32).astype(o_ref.dtype)


def _norm_matmul(x, gain, w, w_aux=None, *, out_dtype, emit_xn=False, tm=1024, tn=1024):
    T, D = x.shape
    N = w.shape[1]
    tm = min(tm, T)
    tn = min(tn, N)
    has_aux = w_aux is not None
    in_specs = [pl.BlockSpec((tm, D), lambda i, j: (i, 0)),
                pl.BlockSpec((1, D), lambda i, j: (0, 0)),
                pl.BlockSpec((D, tn), lambda i, j: (0, j))]
    args = [x, gain.reshape(1, D), w]
    out_shape = [jax.ShapeDtypeStruct((T, N), out_dtype)]
    out_specs = [pl.BlockSpec((tm, tn), lambda i, j: (i, j))]
    if has_aux:
        in_specs.append(pl.BlockSpec((D, LANES), lambda i, j: (0, 0)))
        args.append(w_aux)
        out_shape.append(jax.ShapeDtypeStruct((T, LANES), f32))
        out_specs.append(pl.BlockSpec((tm, LANES), lambda i, j: (i, 0)))
    if emit_xn:
        out_shape.append(jax.ShapeDtypeStruct((T, D), bf16))
        out_specs.append(pl.BlockSpec((tm, D), lambda i, j: (i, 0)))
    return pl.pallas_call(
        functools.partial(_norm_matmul_kernel, has_aux=has_aux, emit_xn=emit_xn),
        grid=(T // tm, N // tn),
        in_specs=in_specs, out_specs=out_specs, out_shape=out_shape,
        scratch_shapes=[pltpu.VMEM((tm, D), bf16)],
        compiler_params=_cparams(("parallel", "arbitrary"), 48),
        name="norm_matmul",
    )(*args)


def _matmul_res_kernel(*refs, n_in):
    a_refs = refs[:n_in]
    w_refs = refs[n_in:2 * n_in]
    r_ref = refs[2 * n_in]
    o_ref = refs[2 * n_in + 1]
    acc = r_ref[...]
    for a_ref, w_ref in zip(a_refs, w_refs):
        acc = acc + jnp.dot(a_ref[...], w_ref[...], preferred_element_type=f32)
    o_ref[...] = acc


def _matmul_res(a_list, w_list, res, *, tm=1024, tn=512):
    T, N = res.shape
    tm = min(tm, T)
    tn = min(tn, N)
    n_in = len(a_list)
    in_specs = ([pl.BlockSpec((tm, a.shape[1]), lambda i, j: (i, 0)) for a in a_list]
                + [pl.BlockSpec((w.shape[0], tn), lambda i, j: (0, j)) for w in w_list]
                + [pl.BlockSpec((tm, tn), lambda i, j: (i, j))])
    return pl.pallas_call(
        functools.partial(_matmul_res_kernel, n_in=n_in),
        grid=(T // tm, N // tn),
        in_specs=in_specs,
        out_specs=pl.BlockSpec((tm, tn), lambda i, j: (i, j)),
        out_shape=jax.ShapeDtypeStruct((T, N), f32),
        compiler_params=_cparams(("parallel", "parallel"), 48),
        name="matmul_res",
    )(*a_list, *w_list, res)


def _retention_kernel(q_ref, k_ref, v_ref, g_ref, cos_ref, sin_ref, lg_ref, nw_ref, o_ref, r_ref):
    @pl.when(pl.program_id(2) == 0)
    def _():
        r_ref[...] = jnp.zeros_like(r_ref)

    n = q_ref.shape[1]
    half = HEAD_DIM // 2
    cos = cos_ref[...]
    sin = sin_ref[...]

    def rot(t):
        t1, t2 = t[:, :half], t[:, half:]
        return jnp.concatenate([t1 * cos - t2 * sin, t1 * sin + t2 * cos], axis=-1)

    q = rot(q_ref[0].astype(f32))
    k = rot(k_ref[0].astype(f32)) * (HEAD_DIM ** -0.5)
    v = v_ref[0]
    lg = lg_ref[0]
    lg1 = lg[:, :1]
    ti = lax.broadcasted_iota(jnp.int32, (n, n), 0)
    si = lax.broadcasted_iota(jnp.int32, (n, n), 1)
    rel = (ti - si).astype(f32)
    decay = jnp.where(rel >= 0, jnp.exp(rel * lg), 0.0)
    idx = lax.broadcasted_iota(jnp.int32, (n, 1), 0).astype(f32)
    q_decay = jnp.exp((idx + 1.0) * lg1)
    k_decay = jnp.exp((n - 1.0 - idx) * lg1)
    chunk_decay = jnp.exp(float(n) * lg1)
    qb = q.astype(bf16)
    inner = lax.dot_general(qb, k.astype(bf16), _TN, preferred_element_type=f32) * decay
    r = r_ref[...]
    o = (jnp.dot(inner.astype(bf16), v, preferred_element_type=f32)
         + jnp.dot(qb, r.astype(bf16), preferred_element_type=f32) * q_decay)
    r_ref[...] = r * chunk_decay + lax.dot_general((k * k_decay).astype(bf16), v, _TM,
                                                   preferred_element_type=f32)
    g = g_ref[0].astype(f32)
    o_ref[0] = (_group_norm(o, nw_ref[...]) * _silu(g)).astype(o_ref.dtype)


def _retention(proj, cos, sin, log_g, norm_w):
    B, S, _ = proj.shape
    nC = S // CHUNK
    H = RET_HEADS
    blk = lambda off: pl.BlockSpec((1, CHUNK, HEAD_DIM), lambda b, h, c: (b, c, off + h))
    return pl.pallas_call(
        _retention_kernel,
        grid=(B, H, nC),
        in_specs=[blk(0), blk(H), blk(2 * H), blk(3 * H),
                  pl.BlockSpec((CHUNK, HEAD_DIM // 2), lambda b, h, c: (c, 0)),
                  pl.BlockSpec((CHUNK, HEAD_DIM // 2), lambda b, h, c: (c, 0)),
                  pl.BlockSpec((1, 1, CHUNK), lambda b, h, c: (h, 0, 0)),
                  pl.BlockSpec((1, HEAD_DIM), lambda b, h, c: (0, h))],
        out_specs=pl.BlockSpec((1, CHUNK, HEAD_DIM), lambda b, h, c: (b, c, h)),
        out_shape=jax.ShapeDtypeStruct((B, S, H * HEAD_DIM), bf16),
        scratch_shapes=[pltpu.VMEM((HEAD_DIM, HEAD_DIM), f32)],
        compiler_params=_cparams(("parallel", "parallel", "arbitrary"), 32),
        name="retention",
    )(proj, proj, proj, proj, cos, sin, log_g, norm_w)


def _mlstm_kernel(q_ref, k_ref, v_ref, og_ref, gc_ref, gr_ref, bc_ref, br_ref,
                  cwq_ref, cwk_ref, cbq_ref, cbk_ref, nw_ref, o_ref,
                  c_ref, n_ref, m_ref, hq_ref, hk_ref):
    @pl.when(pl.program_id(2) == 0)
    def _():
        c_ref[...] = jnp.zeros_like(c_ref)
        n_ref[...] = jnp.zeros_like(n_ref)
        m_ref[...] = jnp.full(m_ref.shape, -1e30, f32)
        hq_ref[0:CONV_PAD, :] = jnp.zeros((CONV_PAD, HEAD_DIM), f32)
        hk_ref[0:CONV_PAD, :] = jnp.zeros((CONV_PAD, HEAD_DIM), f32)

    n = q_ref.shape[1]
    q = _silu(_causal_conv(q_ref[0].astype(f32), hq_ref, cwq_ref[...], cbq_ref[...]))
    k = _silu(_causal_conv(k_ref[0].astype(f32), hk_ref, cwk_ref[...], cbk_ref[...])) * (HEAD_DIM ** -0.5)
    v = v_ref[0]
    gc = gc_ref[0, 0] + bc_ref[0]
    gr = gr_ref[0, 0] + br_ref[0]
    i_c = gc[:, 0:1]
    lf_c = jax.nn.log_sigmoid(gc[:, 1:2])
    i_r = gr[0:1, :]
    lf_r = jax.nn.log_sigmoid(gr[1:2, :])
    tril = _tril_mask(n)
    b_c = jnp.sum(jnp.where(tril, lf_r, 0.0), axis=-1, keepdims=True)
    b_r = jnp.sum(jnp.where(_triu_mask(n), lf_c, 0.0), axis=0, keepdims=True)
    m_prev = m_ref[...]
    log_intra = jnp.where(tril, b_c - b_r + i_r, -jnp.inf)
    log_inter = b_c + m_prev
    m_t = jnp.maximum(log_inter, jnp.max(log_intra, axis=-1, keepdims=True))
    w_intra = jnp.exp(log_intra - m_t)
    w_inter = jnp.exp(log_inter - m_t)
    qb = q.astype(bf16)
    s = lax.dot_general(qb, k.astype(bf16), _TN, preferred_element_type=f32) * w_intra
    c_st = c_ref[...]
    n_st = n_ref[...]
    num = (jnp.dot(s.astype(bf16), v, preferred_element_type=f32)
           + w_inter * jnp.dot(qb, c_st.astype(bf16), preferred_element_type=f32))
    den = (jnp.sum(s, axis=-1, keepdims=True)
           + w_inter * jnp.sum(qb.astype(f32) * n_st, axis=-1, keepdims=True))
    hh = num / jnp.maximum(jnp.abs(den), jnp.exp(-m_t))
    m_new = m_t[n - 1:n, :]
    b_last = b_c[n - 1:n, :]
    w_end = jnp.exp(b_last - b_c + i_c - m_new)
    w_prev = jnp.exp(b_last + m_prev - m_new)
    kw = k * w_end
    c_ref[...] = w_prev * c_st + lax.dot_general(kw.astype(bf16), v, _TM, preferred_element_type=f32)
    n_ref[...] = w_prev * n_st + jnp.sum(kw, axis=0, keepdims=True)
    m_ref[...] = m_new
    og = og_ref[0].astype(f32)
    o_ref[0] = (_group_norm(hh, nw_ref[...]) * jax.nn.sigmoid(og)).astype(o_ref.dtype)


def _mlstm(proj, gates_c, gates_r, bias_c, bias_r, conv_w, conv_b, norm_w):
    B, S, _ = proj.shape
    nC = S // CHUNK
    H = MLSTM_HEADS
    base = 4 * RET_HEADS
    blk = lambda off: pl.BlockSpec((1, CHUNK, HEAD_DIM), lambda b, h, c: (b, c, base + off + h))
    return pl.pallas_call(
        _mlstm_kernel,
        grid=(B, H, nC),
        in_specs=[blk(0), blk(H), blk(2 * H), blk(3 * H),
                  pl.BlockSpec((1, 1, CHUNK, 2), lambda b, h, c: (b, h, c, 0)),
                  pl.BlockSpec((1, 1, 2, CHUNK), lambda b, h, c: (b, h, 0, c)),
                  pl.BlockSpec((1, 1, 2), lambda b, h, c: (h, 0, 0)),
                  pl.BlockSpec((1, 2, 1), lambda b, h, c: (h, 0, 0)),
                  pl.BlockSpec((CONV_K, HEAD_DIM), lambda b, h, c: (0, h)),
                  pl.BlockSpec((CONV_K, HEAD_DIM), lambda b, h, c: (0, H + h)),
                  pl.BlockSpec((1, HEAD_DIM), lambda b, h, c: (0, h)),
                  pl.BlockSpec((1, HEAD_DIM), lambda b, h, c: (0, H + h)),
                  pl.BlockSpec((1, HEAD_DIM), lambda b, h, c: (0, h))],
        out_specs=pl.BlockSpec((1, CHUNK, HEAD_DIM), lambda b, h, c: (b, c, h)),
        out_shape=jax.ShapeDtypeStruct((B, S, H * HEAD_DIM), bf16),
        scratch_shapes=[pltpu.VMEM((HEAD_DIM, HEAD_DIM), f32),
                        pltpu.VMEM((1, HEAD_DIM), f32),
                        pltpu.VMEM((1, 1), f32),
                        pltpu.VMEM((CHUNK + CONV_PAD, HEAD_DIM), f32),
                        pltpu.VMEM((CHUNK + CONV_PAD, HEAD_DIM), f32)],
        compiler_params=_cparams(("parallel", "parallel", "arbitrary"), 32),
        name="mlstm",
    )(proj, proj, proj, proj, gates_c, gates_r, bias_c, bias_r, conv_w, conv_w, conv_b, conv_b, norm_w)


def _ssd_kernel(z_ref, x_ref, b_ref, c_ref, dtc_ref, dtr_ref,
                cwx_ref, cwb_ref, cwc_ref, cbx_ref, cbb_ref, cbc_ref,
                dbr_ref, dbc_ref, alr_ref, alc_ref, dsk_ref, nw_ref, ex_ref, o_ref,
                st_ref, hx_ref, hb_ref, hc_ref):
    @pl.when(pl.program_id(2) == 0)
    def _():
        st_ref[...] = jnp.zeros_like(st_ref)
        hx_ref[0:CONV_PAD, :] = jnp.zeros((CONV_PAD, SSM_GW), f32)
        hb_ref[0:CONV_PAD, :] = jnp.zeros((CONV_PAD, SSM_N), f32)
        hc_ref[0:CONV_PAD, :] = jnp.zeros((CONV_PAD, SSM_N), f32)

    n = x_ref.shape[1]
    x = _silu(_causal_conv(x_ref[0].astype(f32), hx_ref, cwx_ref[...], cbx_ref[...]))
    bm = _silu(_causal_conv(b_ref[0].astype(f32), hb_ref, cwb_ref[...], cbb_ref[...]))
    cm = _silu(_causal_conv(c_ref[0].astype(f32), hc_ref, cwc_ref[...], cbc_ref[...]))
    bmb = bm.astype(bf16)
    cmb = cm.astype(bf16)

    dt_c = jax.nn.softplus(dtc_ref[0, 0] + dbr_ref[0])
    a_c = dt_c * (-jnp.exp(alr_ref[0]))
    dt_r = jax.nn.softplus(dtr_ref[0, 0] + dbc_ref[0])
    a_r = dt_r * (-jnp.exp(alc_ref[0]))
    tril = _tril_mask(n)
    lower = tril.astype(bf16)
    upper = _triu_mask(n).astype(bf16)
    acum_c = _split_dot(lower, a_c, 3, split_rhs=True)
    acum_r = _split_dot(a_r, upper, 3)
    last = acum_c[n - 1:n, :]
    ex = ex_ref[...]
    full = _split_dot(jnp.concatenate([dt_c, jnp.exp(acum_c), jnp.exp(last - acum_c),
                                       jnp.broadcast_to(jnp.exp(last), (8, SSM_HPG))], axis=0), ex, 2)
    dt_f = full[0:n]
    ea_f = full[n:2 * n]
    ew_f = full[2 * n:3 * n]
    el_f = full[3 * n:3 * n + 1]
    xdt = x * dt_f

    cb = lax.dot_general(cmb, bmb, _TN, preferred_element_type=f32)
    lane = lax.broadcasted_iota(jnp.int32, (n, LANES), 1)
    ys = []
    for pair in range(SSM_GW // LANES):
        xp = xdt[:, pair * LANES:(pair + 1) * LANES]
        acc = None
        for sub in range(LANES // SSM_P):
            jh = pair * (LANES // SSM_P) + sub
            dec = jnp.exp(jnp.where(tril, acum_c[:, jh:jh + 1] - acum_r[jh:jh + 1, :], -jnp.inf))
            sel = (lane >= sub * SSM_P) & (lane < (sub + 1) * SSM_P)
            y = jnp.dot((cb * dec).astype(bf16), jnp.where(sel, xp, 0.0).astype(bf16),
                        preferred_element_type=f32)
            acc = y if acc is None else acc + y
        ys.append(acc)
    y_diag = jnp.concatenate(ys, axis=-1)

    st = st_ref[...]
    y_off = jnp.dot(cmb, st.astype(bf16), preferred_element_type=f32) * ea_f
    xw = xdt * ew_f
    st_ref[...] = st * el_f + lax.dot_general(bmb, xw.astype(bf16), _TM, preferred_element_type=f32)

    y = y_diag + y_off + x * dsk_ref[...]
    y = y * _silu(z_ref[0].astype(f32))
    o_ref[0] = _rms(y, nw_ref[...]).astype(o_ref.dtype)


def _ssd(proj, dt_c, dt_r, conv_w, conv_b, dtb_r, dtb_c, al_r, al_c, dskip, norm_w, expand):
    B, S, _ = proj.shape
    nC = S // CHUNK
    G = SSM_GROUPS
    inner = G * SSM_GW
    xo = inner // SSM_GW
    bo = 2 * inner // SSM_N
    co = bo + G
    cw_bo = inner // SSM_N
    cw_co = cw_bo + G
    im3 = lambda off: (lambda b, g, c: (b, c, off + g))
    im2 = lambda off: (lambda b, g, c: (0, off + g))
    gvec = lambda shape: pl.BlockSpec((1,) + shape, lambda b, g, c: (g, 0, 0))
    return pl.pallas_call(
        _ssd_kernel,
        grid=(B, G, nC),
        in_specs=[pl.BlockSpec((1, CHUNK, SSM_GW), im3(0)),
                  pl.BlockSpec((1, CHUNK, SSM_GW), im3(xo)),
                  pl.BlockSpec((1, CHUNK, SSM_N), im3(bo)),
                  pl.BlockSpec((1, CHUNK, SSM_N), im3(co)),
                  pl.BlockSpec((1, 1, CHUNK, SSM_HPG), lambda b, g, c: (b, g, c, 0)),
                  pl.BlockSpec((1, 1, SSM_HPG, CHUNK), lambda b, g, c: (b, g, 0, c)),
                  pl.BlockSpec((CONV_K, SSM_GW), im2(0)),
                  pl.BlockSpec((CONV_K, SSM_N), im2(cw_bo)),
                  pl.BlockSpec((CONV_K, SSM_N), im2(cw_co)),
                  pl.BlockSpec((1, SSM_GW), im2(0)),
                  pl.BlockSpec((1, SSM_N), im2(cw_bo)),
                  pl.BlockSpec((1, SSM_N), im2(cw_co)),
                  gvec((1, SSM_HPG)), gvec((SSM_HPG, 1)), gvec((1, SSM_HPG)), gvec((SSM_HPG, 1)),
                  pl.BlockSpec((1, SSM_GW), im2(0)),
                  pl.BlockSpec((1, SSM_GW), im2(0)),
                  pl.BlockSpec((SSM_HPG, SSM_GW), lambda b, g, c: (0, 0))],
        out_specs=pl.BlockSpec((1, CHUNK, SSM_GW), lambda b, g, c: (b, c, g)),
        out_shape=jax.ShapeDtypeStruct((B, S, inner), bf16),
        scratch_shapes=[pltpu.VMEM((SSM_N, SSM_GW), f32),
                        pltpu.VMEM((CHUNK + CONV_PAD, SSM_GW), f32),
                        pltpu.VMEM((CHUNK + CONV_PAD, SSM_N), f32),
                        pltpu.VMEM((CHUNK + CONV_PAD, SSM_N), f32)],
        compiler_params=_cparams(("parallel", "parallel", "arbitrary"), 32),
        name="ssd",
    )(proj, proj, proj, proj, dt_c, dt_r, conv_w, conv_w, conv_w, conv_b, conv_b, conv_b,
      dtb_r, dtb_c, al_r, al_c, dskip, norm_w, expand)


def _pack_rows(x):
    return pltpu.bitcast(x.astype(bf16), jnp.uint32)


def _unpack_rows(x):
    return pltpu.bitcast(x, bf16)


def _top_values(x, count):
    vals = []
    rank = jnp.full(x.shape, float(count), f32)
    for r in range(count):
        m = jnp.max(x, axis=0, keepdims=True)
        vals.append(m)
        hit = x == m
        rank = jnp.where(hit, float(r), rank)
        x = jnp.where(hit, -jnp.inf, x)
    return vals, rank


def _peer_route_kernel(q_ref, keys_ref, r2_ref, e2_ref, n1_ref, e1_ref):
    qb = q_ref[...]
    dk = keys_ref.shape[-1]
    K = PEER_TOPK
    s1 = lax.dot_general(keys_ref[0, 0], qb[:, :dk], _TN, preferred_element_type=f32)
    s2 = lax.dot_general(keys_ref[0, 1], qb[:, dk:], _TN, preferred_element_type=f32)
    a, rank1 = _top_values(s1, K + 1)
    b, rank2 = _top_values(s2, K + 1)
    bmat = jnp.concatenate(b[:K], axis=0)
    blocks = [a[r] + bmat for r in range(K)]
    cs, _ = _top_values(jnp.concatenate(blocks, axis=0), K + 1)
    dropped = jnp.maximum(cs[K], jnp.maximum(a[K] + b[0], a[0] + b[K]))
    thr = 0.5 * (cs[K - 1] + dropped)
    z = jnp.exp(cs[0] - cs[0])
    for r in range(1, K):
        z = z + jnp.exp(cs[r] - cs[0])
    n_keep = jnp.zeros(s1.shape, f32)
    for r in range(K):
        count_r = jnp.sum((blocks[r] >= thr).astype(f32), axis=0, keepdims=True)
        n_keep = jnp.where(rank1 == float(r), count_r, n_keep)
    r2_ref[0] = _pack_rows(rank2)
    e2_ref[0] = _pack_rows(jnp.exp(s2 - b[0]))
    n1_ref[0] = n_keep
    e1_ref[0] = jnp.exp(s1 - a[0]) / z


def _peer_route(q, keys, *, tb=512):
    T = q.shape[0]
    tb = min(tb, T)
    H = PEER_HEADS
    nk = PEER_NKEYS
    dk = keys.shape[-1]
    ospec = pl.BlockSpec((1, nk, tb), lambda i, h: (h, 0, i))
    pspec = pl.BlockSpec((1, nk // 2, tb), lambda i, h: (h, 0, i))
    packed = jax.ShapeDtypeStruct((H, nk // 2, T), jnp.uint32)
    plain = jax.ShapeDtypeStruct((H, nk, T), f32)
    return pl.pallas_call(
        _peer_route_kernel,
        grid=(T // tb, H),
        in_specs=[pl.BlockSpec((tb, 2 * dk), lambda i, h: (i, h)),
                  pl.BlockSpec((1, 2, nk, dk), lambda i, h: (h, 0, 0, 0))],
        out_specs=[pspec, pspec, ospec, ospec],
        out_shape=[packed, packed, plain, plain],
        compiler_params=_cparams(("parallel", "parallel"), 32),
        name="peer_route",
    )(q, keys)


BF16_ROWS = 16


def _peer_dense_kernel(hn_ref, u_ref, vt_ref, r2_ref, e2_ref, n1_ref, e1_ref, o_ref,
                       acc_ref, at_ref, w_ref, *, n1):
    j = pl.program_id(1)

    @pl.when(j == 0)
    def _():
        acc_ref[...] = jnp.zeros_like(acc_ref)

    at_ref[...] = lax.dot_general(u_ref[...], hn_ref[...], _TN, preferred_element_type=f32)
    tb = hn_ref.shape[0]
    nk = PEER_NKEYS
    half = BF16_ROWS // 2
    zero = jnp.zeros((), bf16)
    for il in range(n1):
        for ct in range(tb // LANES):
            cols = slice(ct * LANES, (ct + 1) * LANES)
            widen = lambda ref, h: jnp.broadcast_to(ref[h, il:il + 1, cols], (BF16_ROWS, LANES)).astype(bf16)
            keep = [widen(n1_ref, h) for h in range(PEER_HEADS)]
            scale = [widen(e1_ref, h) for h in range(PEER_HEADS)]
            for rg in range(nk // BF16_ROWS):
                prows = slice(rg * half, (rg + 1) * half)
                row0 = il * nk + rg * BF16_ROWS
                g = None
                for h in range(PEER_HEADS):
                    rank = _unpack_rows(r2_ref[h, prows, cols])
                    term = jnp.where(rank < keep[h], _unpack_rows(e2_ref[h, prows, cols]) * scale[h], zero)
                    g = term if g is None else g + term
                a = at_ref[row0:row0 + BF16_ROWS, cols]
                gelu = 0.5 * a * (1.0 + lax.erf(a * (2.0 ** -0.5)))
                w_ref[row0 // 2:row0 // 2 + half, cols] = pltpu.bitcast(gelu.astype(bf16) * g, jnp.uint32)
    wt = _unpack_rows(w_ref[...])
    hd = acc_ref.shape[0] // 2
    for part in range(2):
        rows = slice(part * hd, (part + 1) * hd)
        acc_ref[rows, :] += jnp.dot(vt_ref[rows, :], wt, preferred_element_type=f32)

    @pl.when(j == pl.num_programs(1) - 1)
    def _():
        o_ref[...] = acc_ref[...].T


def _peer_dense(hn, u, vt, r2, e2, n_keep, e1, *, tb=512, n1=8):
    T, D = hn.shape
    tb = min(tb, T)
    nk = PEER_NKEYS
    te = n1 * nk
    H = PEER_HEADS
    full = pl.BlockSpec((H, nk // 2, tb), lambda i, j: (0, 0, i))
    part = pl.BlockSpec((H, n1, tb), lambda i, j: (0, j, i))
    return pl.pallas_call(
        functools.partial(_peer_dense_kernel, n1=n1),
        grid=(T // tb, nk // n1),
        in_specs=[pl.BlockSpec((tb, D), lambda i, j: (i, 0)),
                  pl.BlockSpec((te, D), lambda i, j: (j, 0)),
                  pl.BlockSpec((D, te), lambda i, j: (0, j)),
                  full, full, part, part],
        out_specs=pl.BlockSpec((tb, D), lambda i, j: (i, 0)),
        out_shape=jax.ShapeDtypeStruct((T, D), f32),
        scratch_shapes=[pltpu.VMEM((D, tb), f32),
                        pltpu.VMEM((te, tb), f32),
                        pltpu.VMEM((te // 2, tb), jnp.uint32)],
        compiler_params=_cparams(("parallel", "arbitrary"), 56),
        name="peer_dense",
    )(hn, u, vt, r2, e2, n_keep, e1)


def _ple_kernel(h_ref, d_ref, p_ref, g_ref, wg_ref, pw_ref, fn_ref, o_ref, *, final, tn):
    h = h_ref[...] + d_ref[...]
    xn = _rms(h, g_ref[...]).astype(bf16)
    pb = p_ref[...].astype(bf16)
    D = h.shape[1]
    ss = None
    for c in range(D // tn):
        cols = slice(c * tn, (c + 1) * tn)
        gate = jax.nn.sigmoid(jnp.dot(xn, wg_ref[:, cols], preferred_element_type=f32))
        pv = jnp.dot(pb, pw_ref[:, cols], preferred_element_type=f32)
        h2 = h[:, cols] + pv * gate
        o_ref[:, cols] = h2
        if final:
            part = jnp.sum(h2 * h2, axis=-1, keepdims=True)
            ss = part if ss is None else ss + part
    if final:
        o_ref[...] = o_ref[...] * lax.rsqrt(ss * (1.0 / D) + EPS) * fn_ref[...]


def _ple(h, delta, p, gain, wg, pw, final_gain, *, final, tm=256, tn=512):
    T, D = h.shape
    tm = min(tm, T)
    P = p.shape[1]
    row = lambda i: (i, 0)
    fixed = lambda i: (0, 0)
    return pl.pallas_call(
        functools.partial(_ple_kernel, final=final, tn=tn),
        grid=(T // tm,),
        in_specs=[pl.BlockSpec((tm, D), row), pl.BlockSpec((tm, D), row), pl.BlockSpec((tm, P), row),
                  pl.BlockSpec((1, D), fixed), pl.BlockSpec((D, D), fixed), pl.BlockSpec((P, D), fixed),
                  pl.BlockSpec((1, D), fixed)],
        out_specs=pl.BlockSpec((tm, D), row),
        out_shape=jax.ShapeDtypeStruct((T, D), f32),
        compiler_params=_cparams(("parallel",), 48),
        name="ple",
    )(h, delta, p, gain.reshape(1, D), wg, pw, final_gain.reshape(1, D))


def _pad_cols(w, width):
    return jnp.pad(w, ((0, 0), (0, width - w.shape[1])))


def _peer_ffn(h, norm_w, w_q, keys, u, v):
    q, hn = _norm_matmul(h, norm_w, w_q.astype(bf16), out_dtype=bf16, emit_xn=True)
    r2, e2, n_keep, e1 = _peer_route(q, keys.astype(bf16))
    return _peer_dense(hn, u.astype(bf16), v.T.astype(bf16), r2, e2, n_keep, e1)


def _ab_layer(h, B, S, norm_w, w_in, conv_w, conv_b, ret_norm, mlstm_norm, i_bias, f_bias, w_out):
    T, D = h.shape
    width = RET_HEADS * HEAD_DIM
    main = 4 * width + 4 * MLSTM_HEADS * HEAD_DIM
    proj, gates = _norm_matmul(h, norm_w, w_in[:, :main].astype(bf16),
                               _pad_cols(w_in[:, main:], LANES).astype(bf16), out_dtype=bf16)
    proj = proj.reshape(B, S, main)
    half = HEAD_DIM // 2
    inv = ROPE_BASE ** (-jnp.arange(half, dtype=f32) / half)
    ang = jnp.arange(S, dtype=f32)[:, None] * inv[None, :]
    log_g = jnp.log1p(-(2.0 ** (-5.0 - jnp.arange(RET_HEADS, dtype=f32))))
    log_g = jnp.broadcast_to(log_g[:, None, None], (RET_HEADS, 1, CHUNK))
    y_ret = _retention(proj, jnp.cos(ang), jnp.sin(ang), log_g, ret_norm.reshape(1, width))

    H = MLSTM_HEADS
    gi = gates[:, :H].reshape(B, S, H)
    gf = gates[:, H:2 * H].reshape(B, S, H)
    g_c = jnp.stack([gi, gf], axis=-1).transpose(0, 2, 1, 3)
    g_r = jnp.stack([gi, gf], axis=-1).transpose(0, 2, 3, 1)
    bias = jnp.stack([i_bias, f_bias], axis=-1)
    y_ml = _mlstm(proj, g_c, g_r, bias.reshape(H, 1, 2), bias.reshape(H, 2, 1),
                  conv_w, conv_b.reshape(1, -1), mlstm_norm.reshape(1, -1))
    wo = w_out.astype(bf16)
    return _matmul_res([y_ret.reshape(T, width), y_ml.reshape(T, -1)], [wo[:width], wo[width:]], h)


def _ssd_layer(h, B, S, norm_w, w_in, conv_w, conv_b, dt_bias, a_log, d_skip, norm_g, w_out):
    T, D = h.shape
    G, J = SSM_GROUPS, SSM_HPG
    inner = G * SSM_GW
    main = 2 * inner + 2 * G * SSM_N
    proj, dt = _norm_matmul(h, norm_w, w_in[:, :main].astype(bf16),
                            _pad_cols(w_in[:, main:], LANES).astype(bf16), out_dtype=bf16)
    proj = proj.reshape(B, S, main)
    dt = dt[:, :G * J].reshape(B, S, G, J)
    expand = (jnp.arange(SSM_GW)[None, :] // SSM_P == jnp.arange(J)[:, None]).astype(bf16)
    y = _ssd(proj, dt.transpose(0, 2, 1, 3), dt.transpose(0, 2, 3, 1), conv_w, conv_b.reshape(1, -1),
             dt_bias.reshape(G, 1, J), dt_bias.reshape(G, J, 1), a_log.reshape(G, 1, J), a_log.reshape(G, J, 1),
             jnp.repeat(d_skip, SSM_P).reshape(1, inner), norm_g.reshape(1, inner), expand)
    return _matmul_res([y.reshape(T, inner)], [w_out.astype(bf16)], h)


def kernel(x, p, ab_w_in, ab_conv_w, ab_conv_b, ab_ret_norm, ab_mlstm_norm, ab_i_bias, ab_f_bias, ab_w_out, ssd_w_in, ssd_conv_w, ssd_conv_b, ssd_dt_bias, ssd_a_log, ssd_d, ssd_norm, ssd_w_out, norm_mix, norm_ffn, peer_w_q, peer_keys, peer_u, peer_v, ple_norm, ple_gate_w, ple_w, final_norm):
    B, S, D = x.shape
    T = B * S
    depth = p.shape[0]
    h = x.reshape(T, D)
    for layer in range(depth):
        j = layer // 2
        if layer % 2 == 0:
            h = _ab_layer(h, B, S, norm_mix[layer], ab_w_in[j], ab_conv_w[j], ab_conv_b[j], ab_ret_norm[j],
                          ab_mlstm_norm[j], ab_i_bias[j], ab_f_bias[j], ab_w_out[j])
        else:
            h = _ssd_layer(h, B, S, norm_mix[layer], ssd_w_in[j], ssd_conv_w[j], ssd_conv_b[j], ssd_dt_bias[j],
                           ssd_a_log[j], ssd_d[j], ssd_norm[j], ssd_w_out[j])
        delta = _peer_ffn(h, norm_ffn[layer], peer_w_q[layer], peer_keys[layer], peer_u[layer], peer_v[layer])
        h = _ple(h, delta, p[layer].reshape(T, -1), ple_norm[layer], ple_gate_w[layer].astype(bf16),
                 ple_w[layer].astype(bf16), final_norm, final=(layer == depth - 1))
    return h.reshape(B, S, D)
```
